```python
import math
import jax, jax.numpy as jnp
from jax import lax
import numpy as np

D_MODEL = 1024
BATCH = 8
SEQ = 2048
DEPTH = 1
DEC_BATCH = 4
DEC_SEQ = 8192
PAST_LEN = 128

N_DIFF_HEADS = 4
DIFF_HEAD_DIM = 64
ATTN_WIDTH = N_DIFF_HEADS * 2 * DIFF_HEAD_DIM
CONV_WIDTH = D_MODEL - ATTN_WIDTH
MIX_WIDTH = ATTN_WIDTH + CONV_WIDTH
IN_COLS = 3 * ATTN_WIDTH + 3 * CONV_WIDTH
CONV_K = 3
ROPE_THETA = 500000.0
ROPE_DIM = DIFF_HEAD_DIM // 4
Q_BLOCK = 128
LAMBDA_INIT_BASE = 0.8
LAMBDA_INIT_AMP = 0.6
LAMBDA_INIT_RATE = 0.3
N_EXPERTS = 256
TOP_K = 8
N_GROUPS = 8
TOPK_GROUPS = 4
EXPERT_HIDDEN = 256
SHARED_HIDDEN = 256
ROUTED_SCALE = 2.5
ROW_BLOCK = 128
EPS = 1e-6

kernel_name = 'hymba_diffattn_shortconv_moe_encoder'


def rms_norm(x, g):
    xf = x.astype(jnp.float32)
    y = xf * lax.rsqrt(jnp.mean(xf * xf, axis=-1, keepdims=True) + EPS)
    return (y * g.astype(jnp.float32)).astype(x.dtype)


def rope_partial(t, pos):
    half = ROPE_DIM // 2
    inv = ROPE_THETA ** (-jnp.arange(0, ROPE_DIM, 2, dtype=jnp.float32) / ROPE_DIM)
    ang = pos.astype(jnp.float32)[:, None] * inv[None, :]
    cos = jnp.cos(ang)[None, :, None, None, :]
    sin = jnp.sin(ang)[None, :, None, None, :]
    tr = t[..., :ROPE_DIM].astype(jnp.float32)
    x1, x2 = tr[..., :half], tr[..., half:]
    rot = jnp.concatenate([x1 * cos - x2 * sin, x2 * cos + x1 * sin], axis=-1)
    return jnp.concatenate([rot.astype(t.dtype), t[..., ROPE_DIM:]], axis=-1)


def diff_attention(q, k, v, lam, lam_init, g_head):
    B, S = q.shape[0], q.shape[1]
    nb = S // Q_BLOCK
    qb = q.reshape(B, nb, Q_BLOCK, N_DIFF_HEADS, 2, DIFF_HEAD_DIM).transpose(1, 0, 2, 3, 4, 5)
    scale = DIFF_HEAD_DIM ** -0.5

    def one_block(qblk):
        s = jnp.einsum('bqhcd,bkhcd->bhcqk', qblk, k, preferred_element_type=jnp.float32) * scale
        p = jax.nn.softmax(s, axis=-1)
        a = p[:, :, 0] - lam * p[:, :, 1]
        return jnp.einsum('bhqk,bkhe->bqhe', a.astype(v.dtype), v, preferred_element_type=jnp.float32)

    o = lax.map(one_block, qb)
    o = o.transpose(1, 0, 2, 3, 4).reshape(B, S, N_DIFF_HEADS, 2 * DIFF_HEAD_DIM)
    o = o * lax.rsqrt(jnp.mean(o * o, axis=-1, keepdims=True) + EPS) * g_head.astype(jnp.float32)
    o = o * (1.0 - lam_init)
    return o.reshape(B, S, ATTN_WIDTH).astype(v.dtype)


def short_conv(b_gate, c_gate, u, w_conv):
    S = u.shape[1]
    z = c_gate * u
    zp = jnp.pad(z, ((0, 0), (1, 1), (0, 0)))
    conv = zp[:, :S] * w_conv[0] + zp[:, 1:S + 1] * w_conv[1] + zp[:, 2:S + 2] * w_conv[2]
    return b_gate * conv


def token_mixer(h, w_in, w_conv, lam, lam_init, g_head, w_out):
    B, S, _ = h.shape
    A, C = ATTN_WIDTH, CONV_WIDTH
    proj = jnp.dot(h, w_in)
    q, k, v, bg, cg, u = jnp.split(proj, [A, 2 * A, 3 * A, 3 * A + C, 3 * A + 2 * C], axis=-1)
    q = q.reshape(B, S, N_DIFF_HEADS, 2, DIFF_HEAD_DIM)
    k = k.reshape(B, S, N_DIFF_HEADS, 2, DIFF_HEAD_DIM)
    v = v.reshape(B, S, N_DIFF_HEADS, 2 * DIFF_HEAD_DIM)
    pos = jnp.arange(S)
    q = rope_partial(q, pos)
    k = rope_partial(k, pos)
    attn = diff_attention(q, k, v, lam, lam_init, g_head)
    conv = short_conv(bg, cg, u, w_conv)
    return jnp.dot(jnp.concatenate([attn, conv], axis=-1), w_out)


def route(h, w_router, router_bias):
    T = h.shape[0]
    scores = jax.nn.sigmoid(jnp.dot(h, w_router, preferred_element_type=jnp.float32))
    sel = scores + router_bias.astype(jnp.float32)
    per_group = N_EXPERTS // N_GROUPS
    grp_score = lax.top_k(sel.reshape(T, N_GROUPS, per_group), 2)[0].sum(-1)
    _, gidx = lax.top_k(grp_score, TOPK_GROUPS)
    gmask = jnp.any(gidx[..., None] == jnp.arange(N_GROUPS)[None, None, :], axis=-2)
    masked = jnp.where(jnp.repeat(gmask, per_group, axis=-1), sel, -jnp.inf)
    _, idx = lax.top_k(masked, TOP_K)
    w = jnp.take_along_axis(scores, idx, axis=-1)
    w = w / (jnp.sum(w, axis=-1, keepdims=True) + 1e-20) * ROUTED_SCALE
    return idx, w


def routed_experts(h, idx, w, w_gate, w_up, w_down):
    T, D = h.shape
    TK = T * TOP_K
    flat_e = idx.reshape(TK)
    flat_t = jnp.repeat(jnp.arange(T, dtype=jnp.int32), TOP_K)
    flat_w = w.reshape(TK)
    order = jnp.argsort(flat_e)
    se, st, sw = flat_e[order], flat_t[order], flat_w[order]
    counts = jnp.bincount(flat_e, length=N_EXPERTS)
    padded = (counts + ROW_BLOCK - 1) // ROW_BLOCK * ROW_BLOCK
    end_pad = jnp.cumsum(padded)
    start_pad = end_pad - padded
    start = jnp.cumsum(counts) - counts
    dest = start_pad[se] + jnp.arange(TK) - start[se]
    n_blocks = -(-TK // ROW_BLOCK) + N_EXPERTS
    R = n_blocks * ROW_BLOCK
    row_tok = jnp.zeros((R,), jnp.int32).at[dest].set(st)
    row_w = jnp.zeros((R,), jnp.float32).at[dest].set(sw)
    block_e = jnp.minimum(
        jnp.searchsorted(end_pad, jnp.arange(n_blocks) * ROW_BLOCK, side='right'), N_EXPERTS - 1)

    def step(acc, blk):
        e, toks, ws = blk
        xb = h[toks]
        y = jnp.dot(jax.nn.silu(jnp.dot(xb, w_gate[e])) * jnp.dot(xb, w_up[e]), w_down[e])
        return acc.at[toks].add(y.astype(jnp.float32) * ws[:, None]), None

    acc, _ = lax.scan(step, jnp.zeros((T, D), jnp.float32),
                      (block_e, row_tok.reshape(n_blocks, ROW_BLOCK), row_w.reshape(n_blocks, ROW_BLOCK)))
    return acc


def moe_ffn(h, w_router, router_bias, w_exp_gate, w_exp_up, w_exp_down, w_sh_gate, w_sh_up, w_sh_down):
    B, S, D = h.shape
    hf = h.reshape(B * S, D)
    idx, w = route(hf, w_router, router_bias)
    routed = routed_experts(hf, idx, w, w_exp_gate, w_exp_up, w_exp_down)
    shared = jnp.dot(jax.nn.silu(jnp.dot(hf, w_sh_gate)) * jnp.dot(hf, w_sh_up), w_sh_down)
    return (routed + shared.astype(jnp.float32)).astype(h.dtype).reshape(B, S, D)


def trunk(x, c, w_ada, b_ada, g_pre_mix, g_post_mix, g_pre_ffn, g_post_ffn, w_in, w_conv,
          lam_q1, lam_k1, lam_q2, lam_k2, g_head, w_out, w_router, router_bias,
          w_exp_gate, w_exp_up, w_exp_down, w_sh_gate, w_sh_up, w_sh_down):
    for l in range(DEPTH):
        lam_init = LAMBDA_INIT_BASE - LAMBDA_INIT_AMP * math.exp(-LAMBDA_INIT_RATE * l)
        lam = (jnp.exp(jnp.sum(lam_q1[l].astype(jnp.float32) * lam_k1[l].astype(jnp.float32)))
               - jnp.exp(jnp.sum(lam_q2[l].astype(jnp.float32) * lam_k2[l].astype(jnp.float32)))
               + lam_init)
        mod = jnp.dot(jax.nn.silu(c), w_ada[l]) + b_ada[l]
        sh1, sc1, ga1, sh2, sc2, ga2 = jnp.split(mod[:, None, :], 6, axis=-1)
        h = rms_norm(x, g_pre_mix[l]) * (1 + sc1) + sh1
        mix = token_mixer(h, w_in[l], w_conv[l], lam, lam_init, g_head[l], w_out[l])
        x = x + ga1 * rms_norm(mix, g_post_mix[l])
        h = rms_norm(x, g_pre_ffn[l]) * (1 + sc2) + sh2
        f = moe_ffn(h, w_router[l], router_bias[l], w_exp_gate[l], w_exp_up[l], w_exp_down[l],
                    w_sh_gate[l], w_sh_up[l], w_sh_down[l])
        x = x + ga2 * rms_norm(f, g_post_ffn[l])
    return x


def setup_inputs(seed: int = 0) -> dict:
    key = jax.random.key(seed)
    ks = jax.random.split(key, 26)
    f32 = jnp.float32
    D, L, E = D_MODEL, DEPTH, N_EXPERTS
    nrm = lambda k, shape, s: jax.random.normal(k, shape, f32) * s
    gain = lambda k, shape: 1.0 + 0.05 * jax.random.normal(k, shape, f32)
    return {
        'x_prompt': nrm(ks[0], (BATCH, SEQ, D), 1.0),
        'x_sample': nrm(ks[1], (DEC_BATCH, DEC_SEQ, D), 1.0),
        'c_prompt': nrm(ks[2], (BATCH, D), 1.0),
        'c_sample': nrm(ks[3], (DEC_BATCH, D), 1.0),
        'w_ada': nrm(ks[4], (L, D, 6 * D), 0.5 * D ** -0.5),
        'b_ada': nrm(ks[5], (L, 6 * D), 0.01),
        'g_pre_mix': gain(ks[6], (L, D)),
        'g_post_mix': gain(ks[7], (L, D)),
        'g_pre_ffn': gain(ks[8], (L, D)),
        'g_post_ffn': gain(ks[9], (L, D)),
        'w_in': nrm(ks[10], (L, D, IN_COLS), D ** -0.5),
        'w_conv': nrm(ks[11], (L, CONV_K, CONV_WIDTH), CONV_K ** -0.5),
        'lam_q1': nrm(ks[12], (L, DIFF_HEAD_DIM), 0.1),
        'lam_k1': nrm(ks[13], (L, DIFF_HEAD_DIM), 0.1),
        'lam_q2': nrm(ks[14], (L, DIFF_HEAD_DIM), 0.1),
        'lam_k2': nrm(ks[15], (L, DIFF_HEAD_DIM), 0.1),
        'g_head': gain(ks[16], (L, 2 * DIFF_HEAD_DIM)),
        'w_out': nrm(ks[17], (L, MIX_WIDTH, D), MIX_WIDTH ** -0.5),
        'w_router': nrm(ks[18], (L, D, E), D ** -0.5),
        'router_bias': nrm(ks[19], (L, E), 0.01),
        'w_exp_gate': nrm(ks[20], (L, E, D, EXPERT_HIDDEN), D ** -0.5),
        'w_exp_up': nrm(ks[21], (L, E, D, EXPERT_HIDDEN), D ** -0.5),
        'w_exp_down': nrm(ks[22], (L, E, EXPERT_HIDDEN, D), EXPERT_HIDDEN ** -0.5),
        'w_sh_gate': nrm(ks[23], (L, D, SHARED_HIDDEN), D ** -0.5),
        'w_sh_up': nrm(ks[24], (L, D, SHARED_HIDDEN), D ** -0.5),
        'w_sh_down': nrm(ks[25], (L, SHARED_HIDDEN, D), SHARED_HIDDEN ** -0.5),
    }


def reference(x_prompt, x_sample, c_prompt, c_sample, w_ada, b_ada, g_pre_mix, g_post_mix, g_pre_ffn,
              g_post_ffn, w_in, w_conv, lam_q1, lam_k1, lam_q2, lam_k2, g_head, w_out, w_router,
              router_bias, w_exp_gate, w_exp_up, w_exp_down, w_sh_gate, w_sh_up, w_sh_down):
    y_prompt = trunk(x_prompt, c_prompt, w_ada, b_ada, g_pre_mix, g_post_mix, g_pre_ffn, g_post_ffn,
                     w_in, w_conv, lam_q1, lam_k1, lam_q2, lam_k2, g_head, w_out, w_router, router_bias,
                     w_exp_gate, w_exp_up, w_exp_down, w_sh_gate, w_sh_up, w_sh_down)
    y_sample = trunk(x_sample, c_sample, w_ada, b_ada, g_pre_mix, g_post_mix, g_pre_ffn, g_post_ffn,
                     w_in, w_conv, lam_q1, lam_k1, lam_q2, lam_k2, g_head, w_out, w_router, router_bias,
                     w_exp_gate, w_exp_up, w_exp_down, w_sh_gate, w_sh_up, w_sh_down)
    return (y_prompt, y_sample)
```

```python
import functools
import math

import jax
import jax.numpy as jnp
from jax import lax
from jax.experimental import pallas as pl
from jax.experimental.pallas import tpu as pltpu

F32 = jnp.float32
BF16 = jnp.bfloat16
I32 = jnp.int32

D_MODEL = 1024
N_DIFF_HEADS = 4
DIFF_HEAD_DIM = 64
HEAD_WIDTH = 2 * DIFF_HEAD_DIM
ATTN_WIDTH = N_DIFF_HEADS * HEAD_WIDTH
CONV_WIDTH = D_MODEL - ATTN_WIDTH
ROPE_THETA = 500000.0
ROPE_DIM = DIFF_HEAD_DIM // 4
ROPE_HALF = ROPE_DIM // 2
LAMBDA_INIT = 0.8 - 0.6 * math.exp(-0.3 * 0)
N_EXPERTS = 256
TOP_K = 8
N_GROUPS = 8
TOPK_GROUPS = 4
GROUP_SIZE = N_EXPERTS // N_GROUPS
EXPERT_HIDDEN = 256
ROUTED_SCALE = 2.5
EPS = 1e-6
LOG2E = 1.4426950408889634

LANES = 128
SUBLANES = 8
ROW_TILES = D_MODEL // LANES
VMEM_LIMIT = 56 * 1024 * 1024
EXPERT_BLOCK_ROWS = 256


def _params(sem):
    return pltpu.CompilerParams(dimension_semantics=sem, vmem_limit_bytes=VMEM_LIMIT)


def _rms(x, g):
    return x * lax.rsqrt(jnp.mean(x * x, axis=-1, keepdims=True) + EPS) * g


def _to_row_tiles(ref, val):
    for s in range(ROW_TILES):
        ref[:, s, :] = val[:, s * LANES:(s + 1) * LANES]


def _from_row_tiles(ref):
    return jnp.concatenate([ref[:, s, :] for s in range(ROW_TILES)], axis=1)


def _ada_kernel(c_ref, w_ref, b_ref, o_ref):
    c = c_ref[...]
    a = (c * jax.nn.sigmoid(c)).astype(BF16)
    o_ref[...] = jnp.dot(a, w_ref[...].astype(BF16), preferred_element_type=F32) + b_ref[...]


def _ada(c, w_ada, b_ada):
    bp, d = c.shape
    n = w_ada.shape[1]
    tn = 1024
    return pl.pallas_call(
        _ada_kernel,
        grid=(n // tn,),
        in_specs=[pl.BlockSpec((bp, d), lambda j: (0, 0)),
                  pl.BlockSpec((d, tn), lambda j: (0, j)),
                  pl.BlockSpec((1, tn), lambda j: (0, j))],
        out_specs=pl.BlockSpec((bp, tn), lambda j: (0, j)),
        out_shape=jax.ShapeDtypeStruct((bp, n), F32),
        compiler_params=_params(("arbitrary",)),
        name="ada",
    )(c, w_ada, b_ada)


def _rope_tables(seq):
    inv = ROPE_THETA ** (-jnp.arange(0, ROPE_DIM, 2, dtype=F32) / ROPE_DIM)
    ang = jnp.arange(seq, dtype=F32)[:, None] * inv[None, :]
    cos, sin = jnp.cos(ang), jnp.sin(ang)
    pad = DIFF_HEAD_DIM - ROPE_DIM
    ones = jnp.ones((seq, pad), F32)
    zeros_h = jnp.zeros((seq, ROPE_HALF), F32)
    zeros_p = jnp.zeros((seq, pad), F32)
    c = jnp.concatenate([cos, cos, ones], axis=1)
    sn = jnp.concatenate([-sin, zeros_h, zeros_p], axis=1)
    sp = jnp.concatenate([zeros_h, sin, zeros_p], axis=1)
    k_tabs = [jnp.tile(t, (1, HEAD_WIDTH // DIFF_HEAD_DIM)) for t in (c, sn, sp)]
    qscale = DIFF_HEAD_DIM ** -0.5 * LOG2E
    return jnp.stack([t * qscale for t in k_tabs] + k_tabs)


def _inproj_kernel(x_ref, mod_ref, g_ref, w_ref, rope_ref, q_ref, k_ref, v_ref, bg_ref, z_ref):
    h = _rms(x_ref[...], g_ref[...]) * (1.0 + mod_ref[0, 1:2, :]) + mod_ref[0, 0:1, :]
    hb = h.astype(BF16)

    def proj(c):
        return jnp.dot(hb, w_ref[:, c * ATTN_WIDTH:(c + 1) * ATTN_WIDTH], preferred_element_type=F32)

    def rope(r, base):
        outs = []
        for hh in range(N_DIFF_HEADS):
            s = r[:, hh * LANES:(hh + 1) * LANES]
            outs.append(s * rope_ref[base]
                        + pltpu.roll(s, LANES - ROPE_HALF, 1) * rope_ref[base + 1]
                        + pltpu.roll(s, ROPE_HALF, 1) * rope_ref[base + 2])
        return jnp.concatenate(outs, axis=1)

    q_ref[...] = rope(proj(0), 0).astype(BF16)
    k_ref[...] = rope(proj(1), 3).astype(BF16)
    v_ref[...] = proj(2).astype(BF16)
    bg_ref[...] = proj(3)
    z_ref[...] = proj(4) * proj(5)


def _inproj(x2, mod, g_pre, w_in_b, rope, seq, seq0, tm):
    t, d = x2.shape
    tps = seq // tm
    a = ATTN_WIDTH
    row = lambda i: (i, 0)
    return pl.pallas_call(
        _inproj_kernel,
        grid=(t // tm,),
        in_specs=[pl.BlockSpec((tm, d), row),
                  pl.BlockSpec((1, 6, d), lambda i: (seq0 + i // tps, 0, 0)),
                  pl.BlockSpec((1, d), lambda i: (0, 0)),
                  pl.BlockSpec(w_in_b.shape, lambda i: (0, 0)),
                  pl.BlockSpec((6, tm, LANES), lambda i: (0, i % tps, 0))],
        out_specs=[pl.BlockSpec((tm, a), row)] * 5,
        out_shape=[jax.ShapeDtypeStruct((t, a), BF16)] * 3 + [jax.ShapeDtypeStruct((t, a), F32)] * 2,
        compiler_params=_params(("arbitrary",)),
        name="inproj",
    )(x2, mod, g_pre, w_in_b, rope)


def _attn_kernel(lam_ref, gh_ref, q_ref, k_ref, v_ref, o_ref, m_sc, l_sc, acc_sc, *, tk):
    tq = q_ref.shape[1]
    q = q_ref[0].astype(F32)
    lane = lax.broadcasted_iota(I32, q.shape, 1)
    qs = jnp.concatenate([jnp.where(lane < DIFF_HEAD_DIM, q, 0.0),
                          jnp.where(lane >= DIFF_HEAD_DIM, q, 0.0)], axis=0).astype(BF16)
    m_sc[...] = jnp.full(m_sc.shape, -jnp.inf, F32)
    l_sc[...] = jnp.zeros(l_sc.shape, F32)
    acc_sc[...] = jnp.zeros(acc_sc.shape, F32)

    def body(j, carry):
        off = pl.multiple_of(j * tk, tk)
        kb = k_ref[0, pl.ds(off, tk), :]
        vb = v_ref[0, pl.ds(off, tk), :]
        s = lax.dot_general(qs, kb, (((1,), (1,)), ((), ())), preferred_element_type=F32)
        m_old = m_sc[...]
        m_new = jnp.maximum(m_old, jnp.max(s, axis=1, keepdims=True))
        alpha = jnp.exp2(m_old - m_new)
        p = jnp.exp2(s - m_new)
        l_sc[...] = alpha * l_sc[...] + jnp.sum(p, axis=1, keepdims=True)
        acc_sc[...] = alpha * acc_sc[...] + jnp.dot(p.astype(BF16), vb, preferred_element_type=F32)
        m_sc[...] = m_new
        return carry

    lax.fori_loop(0, k_ref.shape[1] // tk, body, 0)

    lam = (jnp.exp(jnp.sum(lam_ref[0:1, :] * lam_ref[1:2, :], axis=1, keepdims=True))
           - jnp.exp(jnp.sum(lam_ref[2:3, :] * lam_ref[3:4, :], axis=1, keepdims=True))
           + LAMBDA_INIT)
    o = acc_sc[0:tq, :] / l_sc[0:tq, :] - lam * (acc_sc[tq:, :] / l_sc[tq:, :])
    o = _rms(o, gh_ref[...]) * (1.0 - LAMBDA_INIT)
    o_ref[0] = o.astype(BF16)


def _attn(lam_vecs, g_head, q, k, v, tq, tk):
    b, s, a = q.shape
    qspec = pl.BlockSpec((1, tq, HEAD_WIDTH), lambda bi, h, i: (bi, i, h))
    kvspec = pl.BlockSpec((1, s, HEAD_WIDTH), lambda bi, h, i: (bi, 0, h))
    return pl.pallas_call(
        functools.partial(_attn_kernel, tk=tk),
        grid=(b, N_DIFF_HEADS, s // tq),
        in_specs=[pl.BlockSpec(lam_vecs.shape, lambda bi, h, i: (0, 0)),
                  pl.BlockSpec(g_head.shape, lambda bi, h, i: (0, 0)),
                  qspec, kvspec, kvspec],
        out_specs=qspec,
        out_shape=jax.ShapeDtypeStruct((b, s, a), BF16),
        scratch_shapes=[pltpu.VMEM((2 * tq, 1), F32), pltpu.VMEM((2 * tq, 1), F32),
                        pltpu.VMEM((2 * tq, HEAD_WIDTH), F32)],
        compiler_params=_params(("arbitrary", "arbitrary", "arbitrary")),
        name="attn",
    )(lam_vecs, g_head, q, k, v)


def _outproj_kernel(attn_ref, bg_ref, z_ref, zp_ref, zn_ref, wc_ref, wo_ref, x_ref, mod_ref, g_ref,
                    x1_ref, h2_ref, h2t_ref, *, tps):
    i = pl.program_id(0)
    tm = z_ref.shape[0]
    z = z_ref[...]
    first = (i % tps) == 0
    last = (i % tps) == tps - 1
    z_before = jnp.where(first, 0.0, zp_ref[SUBLANES - 1:SUBLANES, :])
    z_after = jnp.where(last, 0.0, zn_ref[0:1, :])
    row = lax.broadcasted_iota(I32, (tm, 1), 0)
    zm1 = jnp.where(row == 0, z_before, pltpu.roll(z, 1, 0))
    zp1 = jnp.where(row == tm - 1, z_after, pltpu.roll(z, tm - 1, 0))
    conv = bg_ref[...] * (zm1 * wc_ref[0:1, :] + z * wc_ref[1:2, :] + zp1 * wc_ref[2:3, :])
    mix = (jnp.dot(attn_ref[...], wo_ref[0:ATTN_WIDTH, :], preferred_element_type=F32)
           + jnp.dot(conv.astype(BF16), wo_ref[ATTN_WIDTH:, :], preferred_element_type=F32))
    x1 = x_ref[...] + mod_ref[0, 2:3, :] * _rms(mix, g_ref[0:1, :])
    x1_ref[...] = x1
    h2 = _rms(x1, g_ref[1:2, :]) * (1.0 + mod_ref[0, 4:5, :]) + mod_ref[0, 3:4, :]
    h2_ref[...] = h2.astype(BF16)
    _to_row_tiles(h2t_ref, h2)


def _outproj(attn, bg, z, w_conv, w_out_b, x2, mod, g2, seq, seq0, tm):
    t, d = x2.shape
    tps = seq // tm
    a = ATTN_WIDTH
    hb = tm // SUBLANES
    nh = t // SUBLANES
    row = lambda i: (i, 0)
    return pl.pallas_call(
        functools.partial(_outproj_kernel, tps=tps),
        grid=(t // tm,),
        in_specs=[pl.BlockSpec((tm, a), row),
                  pl.BlockSpec((tm, a), row),
                  pl.BlockSpec((tm, a), row),
                  pl.BlockSpec((SUBLANES, a), lambda i: (jnp.maximum(i * hb - 1, 0), 0)),
                  pl.BlockSpec((SUBLANES, a), lambda i: (jnp.minimum((i + 1) * hb, nh - 1), 0)),
                  pl.BlockSpec(w_conv.shape, lambda i: (0, 0)),
                  pl.BlockSpec(w_out_b.shape, lambda i: (0, 0)),
                  pl.BlockSpec((tm, d), row),
                  pl.BlockSpec((1, 6, d), lambda i: (seq0 + i // tps, 0, 0)),
                  pl.BlockSpec(g2.shape, lambda i: (0, 0))],
        out_specs=[pl.BlockSpec((tm, d), row),
                   pl.BlockSpec((tm, d), row),
                   pl.BlockSpec((tm, ROW_TILES, LANES), lambda i: (i, 0, 0))],
        out_shape=[jax.ShapeDtypeStruct((t, d), F32),
                   jax.ShapeDtypeStruct((t, d), BF16),
                   jax.ShapeDtypeStruct((t, ROW_TILES, LANES), F32)],
        compiler_params=_params(("arbitrary",)),
        name="outproj",
    )(attn, bg, z, z, z, w_conv, w_out_b, x2, mod, g2)


def _router_kernel(h_ref, wr_ref, b_ref, eid_ref, wgt_ref, rank_ref, cnt_ref, base_sc):
    i = pl.program_id(0)
    tt = h_ref.shape[0]

    @pl.when(i == 0)
    def _():
        base_sc[...] = jnp.zeros(base_sc.shape, F32)

    logits = lax.dot_general(wr_ref[...], h_ref[...], (((1,), (1,)), ((), ())), preferred_element_type=F32)
    scores = jax.nn.sigmoid(logits)
    sel = scores + b_ref[...]
    selg = sel.reshape(N_GROUPS, GROUP_SIZE, tt)
    ridx = lax.broadcasted_iota(I32, selg.shape, 1)
    m1 = jnp.max(selg, axis=1, keepdims=True)
    first = jnp.min(jnp.where(selg == m1, ridx, GROUP_SIZE), axis=1, keepdims=True)
    m2 = jnp.max(jnp.where(ridx == first, -jnp.inf, selg), axis=1, keepdims=True)
    gs = (m1 + m2).reshape(N_GROUPS, tt)
    gi = lax.broadcasted_iota(I32, gs.shape, 0)
    beaten = jnp.zeros(gs.shape, I32)
    for g in range(N_GROUPS):
        r = gs[g:g + 1, :]
        beaten = beaten + ((r > gs) | ((r == gs) & (g < gi))).astype(I32)
    gmask = (beaten < TOPK_GROUPS).reshape(N_GROUPS, 1, tt)
    masked = jnp.where(gmask, selg, -jnp.inf).reshape(N_EXPERTS, tt)

    eidx = lax.broadcasted_iota(I32, masked.shape, 0)
    picks, wts = [], []
    chosen = jnp.zeros(masked.shape, F32)
    for _ in range(TOP_K):
        mk = jnp.max(masked, axis=0, keepdims=True)
        pick = jnp.min(jnp.where(masked == mk, eidx, N_EXPERTS), axis=0, keepdims=True)
        hit = eidx == pick
        wts.append(jnp.sum(jnp.where(hit, scores, 0.0), axis=0, keepdims=True))
        masked = jnp.where(hit, -jnp.inf, masked)
        chosen = jnp.where(hit, 1.0, chosen)
        picks.append(pick)
    w = jnp.concatenate(wts, axis=0)
    w = w / (jnp.sum(w, axis=0, keepdims=True) + 1e-20) * ROUTED_SCALE

    t_src = lax.broadcasted_iota(I32, (tt, tt), 0)
    t_dst = lax.broadcasted_iota(I32, (tt, tt), 1)
    before = (t_src < t_dst).astype(BF16)
    prior = jnp.dot(chosen.astype(BF16), before, preferred_element_type=F32) + base_sc[...]
    ranks = [jnp.sum(jnp.where(eidx == pk, prior, 0.0), axis=0, keepdims=True) for pk in picks]

    eid_ref[...] = jnp.concatenate(picks, axis=0)
    wgt_ref[...] = w
    rank_ref[...] = jnp.concatenate(ranks, axis=0).astype(I32)
    base_sc[...] = base_sc[...] + jnp.sum(chosen, axis=1, keepdims=True)
    cnt_ref[...] = base_sc[...]


def _router(h2b, wr_t, bias_col, tt):
    t, d = h2b.shape
    col = lambda i: (0, i)
    return pl.pallas_call(
        _router_kernel,
        grid=(t // tt,),
        in_specs=[pl.BlockSpec((tt, d), lambda i: (i, 0)),
                  pl.BlockSpec(wr_t.shape, lambda i: (0, 0)),
                  pl.BlockSpec(bias_col.shape, lambda i: (0, 0))],
        out_specs=[pl.BlockSpec((TOP_K, tt), col), pl.BlockSpec((TOP_K, tt), col), pl.BlockSpec((TOP_K, tt), col),
                   pl.BlockSpec((N_EXPERTS, 1), lambda i: (0, 0))],
        out_shape=[jax.ShapeDtypeStruct((TOP_K, t), I32), jax.ShapeDtypeStruct((TOP_K, t), F32),
                   jax.ShapeDtypeStruct((TOP_K, t), I32), jax.ShapeDtypeStruct((N_EXPERTS, 1), F32)],
        scratch_shapes=[pltpu.VMEM((N_EXPERTS, 1), F32)],
        compiler_params=_params(("arbitrary",)),
        name="router",
    )(h2b, wr_t, bias_col)


def _experts_kernel(be_ref, nu_ref, tok_hbm, h2t_hbm, wg_ref, wu_ref, wd_ref, y_ref,
                    idx_sm, xbuf, wgb, wub, wdb, isem, xsem, *, te):
    i = pl.program_id(0)
    n_used = nu_ref[0]
    n_steps = pl.num_programs(0)
    slot = i % 2
    nslot = 1 - slot

    def idx_copy(blk, sl):
        return pltpu.make_async_copy(tok_hbm.at[blk], idx_sm.at[sl], isem.at[sl])

    def issue_rows(sl):
        def one(r, c):
            tok = idx_sm[sl, 0, r]
            pltpu.make_async_copy(h2t_hbm.at[tok], xbuf.at[sl, r], xsem.at[sl]).start()
            return c
        lax.fori_loop(0, te, one, 0)

    def wait_rows(sl):
        def one(r, c):
            pltpu.make_async_copy(h2t_hbm.at[0], xbuf.at[sl, r], xsem.at[sl]).wait()
            return c
        lax.fori_loop(0, te, one, 0)

    @pl.when(i == 0)
    def _():
        idx_copy(0, 0).start()
        idx_copy(0, 0).wait()
        issue_rows(0)

    @pl.when(i + 1 < n_used)
    def _():
        c = idx_copy(i + 1, nslot)
        c.start()
        c.wait()
        issue_rows(nslot)

    @pl.when(i < n_used)
    def _():
        e_prev = be_ref[jnp.maximum(i - 1, 0)]
        fresh = (i == 0) | (be_ref[i] != e_prev)

        @pl.when(fresh)
        def _():
            wgb[...] = wg_ref[0].astype(BF16)
            wub[...] = wu_ref[0].astype(BF16)
            wdb[...] = wd_ref[0].astype(BF16)

        wait_rows(slot)
        x = _from_row_tiles(xbuf.at[slot]).astype(BF16)
        g = jnp.dot(x, wgb[...], preferred_element_type=F32)
        u = jnp.dot(x, wub[...], preferred_element_type=F32)
        a = (g * jax.nn.sigmoid(g) * u).astype(BF16)
        y = jnp.dot(a, wdb[...], preferred_element_type=F32)
        _to_row_tiles(y_ref, y)

    @pl.when(i >= n_used)
    def _():
        y_ref[...] = jnp.zeros(y_ref.shape, F32)


def _experts(block_e, n_used, tok_rows, h2t3, w_gate, w_up, w_down, te):
    n_blocks = tok_rows.shape[0]
    d, hdn = w_gate.shape[1], w_gate.shape[2]
    wmap = lambda i, be, nu: (be[i], 0, 0)
    grid_spec = pltpu.PrefetchScalarGridSpec(
        num_scalar_prefetch=2,
        grid=(n_blocks,),
        in_specs=[pl.BlockSpec(memory_space=pl.ANY),
                  pl.BlockSpec(memory_space=pl.ANY),
                  pl.BlockSpec((1, d, hdn), wmap),
                  pl.BlockSpec((1, d, hdn), wmap),
                  pl.BlockSpec((1, hdn, d), wmap)],
        out_specs=pl.BlockSpec((te, ROW_TILES, LANES), lambda i, be, nu: (i, 0, 0)),
        scratch_shapes=[pltpu.SMEM((2, 1, te), I32),
                        pltpu.VMEM((2, te, ROW_TILES, LANES), F32),
                        pltpu.VMEM((d, hdn), BF16), pltpu.VMEM((d, hdn), BF16), pltpu.VMEM((hdn, d), BF16),
                        pltpu.SemaphoreType.DMA((2,)), pltpu.SemaphoreType.DMA((2,))],
    )
    return pl.pallas_call(
        functools.partial(_experts_kernel, te=te),
        grid_spec=grid_spec,
        out_shape=jax.ShapeDtypeStruct((n_blocks * te, ROW_TILES, LANES), F32),
        compiler_params=_params(("arbitrary",)),
        name="experts",
    )(block_e, n_used, tok_rows, h2t3, w_gate, w_up, w_down)


def _combine_kernel(dest_hbm, y_hbm, w_ref, h_ref, sg_ref, su_ref, sd_ref, x1_ref, mod_ref, g_ref, o_ref,
                    idx_sm, ybuf, isem, ysem, *, tc):
    i = pl.program_id(0)
    n_steps = pl.num_programs(0)
    slot = i % 2
    nslot = 1 - slot
    n_rows = TOP_K * tc

    def idx_copy(blk, sl):
        return pltpu.make_async_copy(dest_hbm.at[blk], idx_sm.at[sl], isem.at[sl])

    def issue_rows(sl):
        def one(r, c):
            src = idx_sm[sl, 0, r]
            pltpu.make_async_copy(y_hbm.at[src], ybuf.at[sl, r], ysem.at[sl]).start()
            return c
        lax.fori_loop(0, n_rows, one, 0)

    def wait_rows(sl):
        def one(r, c):
            pltpu.make_async_copy(y_hbm.at[0], ybuf.at[sl, r], ysem.at[sl]).wait()
            return c
        lax.fori_loop(0, n_rows, one, 0)

    @pl.when(i == 0)
    def _():
        idx_copy(0, 0).start()
        idx_copy(0, 0).wait()
        issue_rows(0)

    @pl.when(i + 1 < n_steps)
    def _():
        c = idx_copy(i + 1, nslot)
        c.start()
        c.wait()
        issue_rows(nslot)

    hb = h_ref[...]
    sg = jnp.dot(hb, sg_ref[...], preferred_element_type=F32)
    su = jnp.dot(hb, su_ref[...], preferred_element_type=F32)
    f = jnp.dot((sg * jax.nn.sigmoid(sg) * su).astype(BF16), sd_ref[...], preferred_element_type=F32)

    wait_rows(slot)
    yv = ybuf.at[slot]
    w = w_ref[...]
    for k in range(TOP_K):
        f = f + _from_row_tiles(yv.at[k * tc:(k + 1) * tc]) * w[:, k:k + 1]
    o_ref[...] = x1_ref[...] + mod_ref[0, 5:6, :] * _rms(f, g_ref[...])


def _combine(dest_rows, y3, w_tok, h2b, sg_b, su_b, sd_b, x1, mod, g_post, seq_of_tile, tc):
    t, d = x1.shape
    row = lambda i: (i, 0)
    full = lambda i: (0, 0)
    return pl.pallas_call(
        functools.partial(_combine_kernel, tc=tc),
        grid=(t // tc,),
        in_specs=[pl.BlockSpec(memory_space=pl.ANY),
                  pl.BlockSpec(memory_space=pl.ANY),
                  pl.BlockSpec((tc, TOP_K), row),
                  pl.BlockSpec((tc, d), row),
                  pl.BlockSpec(sg_b.shape, full),
                  pl.BlockSpec(su_b.shape, full),
                  pl.BlockSpec(sd_b.shape, full),
                  pl.BlockSpec((tc, d), row),
                  pl.BlockSpec((1, 6, d), lambda i: (seq_of_tile(i), 0, 0)),
                  pl.BlockSpec(g_post.shape, full)],
        out_specs=pl.BlockSpec((tc, d), row),
        out_shape=jax.ShapeDtypeStruct((t, d), F32),
        scratch_shapes=[pltpu.SMEM((2, 1, TOP_K * tc), I32),
                        pltpu.VMEM((2, TOP_K * tc, ROW_TILES, LANES), F32),
                        pltpu.SemaphoreType.DMA((2,)), pltpu.SemaphoreType.DMA((2,))],
        compiler_params=_params(("arbitrary",)),
        name="combine",
    )(dest_rows, y3, w_tok, h2b, sg_b, su_b, sd_b, x1, mod, g_post)


def _pick(n, pref):
    t = min(n, pref)
    assert n % t == 0, (n, pref)
    return t


def _layer(xs, cs, w_ada, b_ada, g_pre_mix, g_post_mix, g_pre_ffn, g_post_ffn, w_in, w_conv,
           lam_q1, lam_k1, lam_q2, lam_k2, g_head, w_out, w_router, router_bias,
           w_exp_gate, w_exp_up, w_exp_down, w_sh_gate, w_sh_up, w_sh_down):
    d = D_MODEL
    n_seq = sum(c.shape[0] for c in cs)
    pad = -n_seq % SUBLANES
    c_all = jnp.concatenate(list(cs) + [jnp.zeros((pad, d), F32)], axis=0)
    mod = _ada(c_all, w_ada, b_ada.reshape(1, -1)).reshape(n_seq + pad, 6, d)

    w_in_b = w_in.astype(BF16)
    w_out_b = w_out.astype(BF16)
    lam_vecs = jnp.stack([lam_q1, lam_k1, lam_q2, lam_k2]).astype(F32)
    g_pre = g_pre_mix.reshape(1, d)
    g2 = jnp.stack([g_post_mix, g_pre_ffn])
    g_hd = g_head.reshape(1, HEAD_WIDTH)

    x1s, h2bs, h2ts = [], [], []
    seq0 = 0
    for x in xs:
        b, s, _ = x.shape
        x2 = x.reshape(b * s, d)
        tm = _pick(s, 512)
        q, k, v, bg, z = _inproj(x2, mod, g_pre, w_in_b, _rope_tables(s), s, seq0, tm)
        shp = (b, s, ATTN_WIDTH)
        attn = _attn(lam_vecs, g_hd, q.reshape(shp), k.reshape(shp), v.reshape(shp), _pick(s, 256), _pick(s, 1024))
        x1, h2b, h2t = _outproj(attn.reshape(b * s, ATTN_WIDTH), bg, z, w_conv, w_out_b, x2, mod, g2, s, seq0, tm)
        x1s.append(x1)
        h2bs.append(h2b)
        h2ts.append(h2t)
        seq0 += b

    x1 = jnp.concatenate(x1s, axis=0)
    h2b = jnp.concatenate(h2bs, axis=0)
    h2t = jnp.concatenate(h2ts, axis=0)
    t = x1.shape[0]

    tt = _pick(t, 512)
    eid, wgt, rank, counts = _router(h2b, w_router.T.astype(BF16), router_bias.reshape(N_EXPERTS, 1), tt)

    te = EXPERT_BLOCK_ROWS
    tk_total = t * TOP_K
    n_blocks = tk_total // te + N_EXPERTS
    counts = counts.reshape(N_EXPERTS).astype(I32)
    padded = (counts + te - 1) // te * te
    end_pad = jnp.cumsum(padded)
    start_pad = end_pad - padded
    dest = start_pad[eid] + rank
    block_e = jnp.minimum(jnp.searchsorted(end_pad, jnp.arange(n_blocks, dtype=I32) * te, side='right'),
                          N_EXPERTS - 1).astype(I32)
    n_used = (end_pad[-1:] // te).astype(I32)
    tok_ids = jnp.broadcast_to(jnp.arange(t, dtype=I32)[None, :], dest.shape)
    tok_rows = jnp.zeros((n_blocks * te,), I32).at[dest.reshape(-1)].set(tok_ids.reshape(-1), unique_indices=True)

    y3 = _experts(block_e, n_used, tok_rows.reshape(n_blocks, 1, te), h2t,
                  w_exp_gate, w_exp_up, w_exp_down, te)

    tc = _pick(t, 128)
    n_tiles = t // tc
    dest_rows = dest.reshape(TOP_K, n_tiles, tc).transpose(1, 0, 2).reshape(n_tiles, 1, TOP_K * tc)
    per0, per1 = xs[0].shape[1] // tc, xs[1].shape[1] // tc
    split = xs[0].shape[0] * per0

    def seq_of_tile(i):
        return jnp.where(i < split, i // per0, xs[0].shape[0] + (i - split) // per1)

    out = _combine(dest_rows, y3, wgt.T, h2b, w_sh_gate.astype(BF16), w_sh_up.astype(BF16), w_sh_down.astype(BF16),
                   x1, mod, g_post_ffn.reshape(1, d), seq_of_tile, tc)

    outs, off = [], 0
    for x in xs:
        n = x.shape[0] * x.shape[1]
        outs.append(out[off:off + n].reshape(x.shape))
        off += n
    return tuple(outs)


def kernel(x_prompt, x_sample, c_prompt, c_sample, w_ada, b_ada, g_pre_mix, g_post_mix, g_pre_ffn, g_post_ffn, w_in, w_conv, lam_q1, lam_k1, lam_q2, lam_k2, g_head, w_out, w_router, router_bias, w_exp_gate, w_exp_up, w_exp_down, w_sh_gate, w_sh_up, w_sh_down):
    return _layer((x_prompt, x_sample), (c_prompt, c_sample), w_ada[0], b_ada[0], g_pre_mix[0], g_post_mix[0],
                  g_pre_ffn[0], g_post_ffn[0], w_in[0], w_conv[0], lam_q1[0], lam_k1[0], lam_q2[0], lam_k2[0],
                  g_head[0], w_out[0], w_router[0], router_bias[0], w_exp_gate[0], w_exp_up[0], w_exp_down[0],
                  w_sh_gate[0], w_sh_up[0], w_sh_down[0])
```

```python
import functools
import math

import jax
import jax.numpy as jnp
from jax import lax
from jax.experimental import pallas as pl
from jax.experimental.pallas import tpu as pltpu

F32 = jnp.float32
BF16 = jnp.bfloat16
I32 = jnp.int32

D_MODEL = 1024
N_DIFF_HEADS = 4
DIFF_HEAD_DIM = 64
HEAD_WIDTH = 2 * DIFF_HEAD_DIM
ATTN_WIDTH = N_DIFF_HEADS * HEAD_WIDTH
CONV_WIDTH = D_MODEL - ATTN_WIDTH
ROPE_THETA = 500000.0
ROPE_DIM = DIFF_HEAD_DIM // 4
ROPE_HALF = ROPE_DIM // 2
LAMBDA_INIT = 0.8 - 0.6 * math.exp(-0.3 * 0)
N_EXPERTS = 256
TOP_K = 8
N_GROUPS = 8
TOPK_GROUPS = 4
GROUP_SIZE = N_EXPERTS // N_GROUPS
EXPERT_HIDDEN = 256
ROUTED_SCALE = 2.5
EPS = 1e-6
LOG2E = 1.4426950408889634

LANES = 128
SUBLANES = 8
ROW_TILES = D_MODEL // LANES
VMEM_LIMIT = 56 * 1024 * 1024
EXPERT_BLOCK_ROWS = 256


def _params(sem):
    return pltpu.CompilerParams(dimension_semantics=sem, vmem_limit_bytes=VMEM_LIMIT)


def _rms(x, g):
    return x * lax.rsqrt(jnp.mean(x * x, axis=-1, keepdims=True) + EPS) * g


def _to_row_tiles(ref, val):
    n = val.shape[0]
    for s in range(ROW_TILES):
        ref[pl.ds(s, n, stride=ROW_TILES), :] = val[:, s * LANES:(s + 1) * LANES]


def _from_row_tiles(ref, first, n):
    return jnp.concatenate([ref[pl.ds(first * ROW_TILES + s, n, stride=ROW_TILES), :] for s in range(ROW_TILES)],
                           axis=1)


def _row_tiles(ref, first, n=1):
    start = first * ROW_TILES
    if not isinstance(start, int):
        start = pl.multiple_of(start, ROW_TILES)
    return ref.at[pl.ds(start, n * ROW_TILES), :]


def _ada_kernel(c_ref, w_ref, b_ref, o_ref):
    c = c_ref[...]
    a = (c * jax.nn.sigmoid(c)).astype(BF16)
    o_ref[...] = jnp.dot(a, w_ref[...].astype(BF16), preferred_element_type=F32) + b_ref[...]


def _ada(c, w_ada, b_ada):
    bp, d = c.shape
    n = w_ada.shape[1]
    tn = 1024
    return pl.pallas_call(
        _ada_kernel,
        grid=(n // tn,),
        in_specs=[pl.BlockSpec((bp, d), lambda j: (0, 0)),
                  pl.BlockSpec((d, tn), lambda j: (0, j)),
                  pl.BlockSpec((1, tn), lambda j: (0, j))],
        out_specs=pl.BlockSpec((bp, tn), lambda j: (0, j)),
        out_shape=jax.ShapeDtypeStruct((bp, n), F32),
        compiler_params=_params(("arbitrary",)),
        name="ada",
    )(c, w_ada, b_ada)


def _rope_tables(seq):
    inv = ROPE_THETA ** (-jnp.arange(0, ROPE_DIM, 2, dtype=F32) / ROPE_DIM)
    ang = jnp.arange(seq, dtype=F32)[:, None] * inv[None, :]
    cos, sin = jnp.cos(ang), jnp.sin(ang)
    pad = DIFF_HEAD_DIM - ROPE_DIM
    ones = jnp.ones((seq, pad), F32)
    zeros_h = jnp.zeros((seq, ROPE_HALF), F32)
    zeros_p = jnp.zeros((seq, pad), F32)
    c = jnp.concatenate([cos, cos, ones], axis=1)
    sn = jnp.concatenate([-sin, zeros_h, zeros_p], axis=1)
    sp = jnp.concatenate([zeros_h, sin, zeros_p], axis=1)
    k_tabs = [jnp.tile(t, (1, HEAD_WIDTH // DIFF_HEAD_DIM)) for t in (c, sn, sp)]
    qscale = DIFF_HEAD_DIM ** -0.5 * LOG2E
    return jnp.stack([t * qscale for t in k_tabs] + k_tabs)


def _inproj_kernel(x_ref, mod_ref, g_ref, w_ref, rope_ref, q_ref, k_ref, v_ref, bg_ref, z_ref):
    h = _rms(x_ref[...], g_ref[...]) * (1.0 + mod_ref[0, 1:2, :]) + mod_ref[0, 0:1, :]
    hb = h.astype(BF16)

    def proj(c):
        return jnp.dot(hb, w_ref[:, c * ATTN_WIDTH:(c + 1) * ATTN_WIDTH], preferred_element_type=F32)

    def rope(r, base):
        outs = []
        for hh in range(N_DIFF_HEADS):
            s = r[:, hh * LANES:(hh + 1) * LANES]
            outs.append(s * rope_ref[base]
                        + pltpu.roll(s, LANES - ROPE_HALF, 1) * rope_ref[base + 1]
                        + pltpu.roll(s, ROPE_HALF, 1) * rope_ref[base + 2])
        return jnp.concatenate(outs, axis=1)

    q_ref[...] = rope(proj(0), 0).astype(BF16)
    k_ref[...] = rope(proj(1), 3).astype(BF16)
    v_ref[...] = proj(2).astype(BF16)
    bg_ref[...] = proj(3)
    z_ref[...] = proj(4) * proj(5)


def _inproj(x2, mod, g_pre, w_in_b, rope, seq, seq0, tm):
    t, d = x2.shape
    tps = seq // tm
    a = ATTN_WIDTH
    row = lambda i: (i, 0)
    return pl.pallas_call(
        _inproj_kernel,
        grid=(t // tm,),
        in_specs=[pl.BlockSpec((tm, d), row),
                  pl.BlockSpec((1, 6, d), lambda i: (seq0 + i // tps, 0, 0)),
                  pl.BlockSpec((1, d), lambda i: (0, 0)),
                  pl.BlockSpec(w_in_b.shape, lambda i: (0, 0)),
                  pl.BlockSpec((6, tm, LANES), lambda i: (0, i % tps, 0))],
        out_specs=[pl.BlockSpec((tm, a), row)] * 5,
        out_shape=[jax.ShapeDtypeStruct((t, a), BF16)] * 3 + [jax.ShapeDtypeStruct((t, a), F32)] * 2,
        compiler_params=_params(("arbitrary",)),
        name="inproj",
    )(x2, mod, g_pre, w_in_b, rope)


def _attn_kernel(lam_ref, gh_ref, q_ref, k_ref, v_ref, o_ref,
                 s0, s1, p0, p1, a0, a1, m_sc, l_sc, acc_sc, *, tk):
    tq = q_ref.shape[1]
    nk = k_ref.shape[1] // tk
    s_bufs, p_bufs, a_bufs = (s0, s1), (p0, p1), (a0, a1)
    q = q_ref[0].astype(F32)
    lane = lax.broadcasted_iota(I32, q.shape, 1)
    qs = jnp.concatenate([jnp.where(lane < DIFF_HEAD_DIM, q, 0.0),
                          jnp.where(lane >= DIFF_HEAD_DIM, q, 0.0)], axis=0).astype(BF16)
    m_sc[...] = jnp.full(m_sc.shape, -jnp.inf, F32)
    l_sc[...] = jnp.zeros(l_sc.shape, F32)
    acc_sc[...] = jnp.zeros(acc_sc.shape, F32)

    def scores(j):
        kb = k_ref[0, j * tk:(j + 1) * tk, :]
        s_bufs[j % 2][...] = lax.dot_general(qs, kb, (((1,), (1,)), ((), ())),
                                             preferred_element_type=F32)

    def softmax(j):
        s = s_bufs[j % 2][...]
        m_old = m_sc[...]
        m_new = jnp.maximum(m_old, jnp.max(s, axis=1, keepdims=True))
        alpha = jnp.exp2(m_old - m_new)
        p = jnp.exp2(s - m_new)
        p_bufs[j % 2][...] = p.astype(BF16)
        a_bufs[j % 2][...] = alpha
        l_sc[...] = alpha * l_sc[...] + jnp.sum(p, axis=1, keepdims=True)
        m_sc[...] = m_new

    def values(j):
        vb = v_ref[0, j * tk:(j + 1) * tk, :]
        acc_sc[...] = a_bufs[j % 2][...] * acc_sc[...] + jnp.dot(p_bufs[j % 2][...], vb, preferred_element_type=F32)

    for step in range(nk + 2):
        if step < nk:
            scores(step)
        if 1 <= step <= nk:
            softmax(step - 1)
        if step >= 2:
            values(step - 2)

    lam = (jnp.exp(jnp.sum(lam_ref[0:1, :] * lam_ref[1:2, :], axis=1, keepdims=True))
           - jnp.exp(jnp.sum(lam_ref[2:3, :] * lam_ref[3:4, :], axis=1, keepdims=True))
           + LAMBDA_INIT)
    o = acc_sc[0:tq, :] / l_sc[0:tq, :] - lam * (acc_sc[tq:, :] / l_sc[tq:, :])
    o = _rms(o, gh_ref[...]) * (1.0 - LAMBDA_INIT)
    o_ref[0] = o.astype(BF16)


def _attn(lam_vecs, g_head, q, k, v, tq, tk):
    b, s, a = q.shape
    qspec = pl.BlockSpec((1, tq, HEAD_WIDTH), lambda bi, h, i: (bi, i, h))
    kvspec = pl.BlockSpec((1, s, HEAD_WIDTH), lambda bi, h, i: (bi, 0, h))
    col = pltpu.VMEM((2 * tq, 1), F32)
    return pl.pallas_call(
        functools.partial(_attn_kernel, tk=tk),
        grid=(b, N_DIFF_HEADS, s // tq),
        in_specs=[pl.BlockSpec(lam_vecs.shape, lambda bi, h, i: (0, 0)),
                  pl.BlockSpec(g_head.shape, lambda bi, h, i: (0, 0)),
                  qspec, kvspec, kvspec],
        out_specs=qspec,
        out_shape=jax.ShapeDtypeStruct((b, s, a), BF16),
        scratch_shapes=[pltpu.VMEM((2 * tq, tk), F32), pltpu.VMEM((2 * tq, tk), F32),
                        pltpu.VMEM((2 * tq, tk), BF16), pltpu.VMEM((2 * tq, tk), BF16),
                        col, col, col, col, pltpu.VMEM((2 * tq, HEAD_WIDTH), F32)],
        compiler_params=_params(("arbitrary", "arbitrary", "arbitrary")),
        name="attn",
    )(lam_vecs, g_head, q, k, v)


def _outproj_kernel(attn_ref, bg_ref, z_ref, zp_ref, zn_ref, wc_ref, wo_ref, x_ref, mod_ref, g_ref,
                    x1_ref, h2_ref, h2t_ref, *, tps):
    i = pl.program_id(0)
    tm = z_ref.shape[0]
    z = z_ref[...]
    first = (i % tps) == 0
    last = (i % tps) == tps - 1
    z_before = jnp.where(first, 0.0, zp_ref[SUBLANES - 1:SUBLANES, :])
    z_after = jnp.where(last, 0.0, zn_ref[0:1, :])
    row = lax.broadcasted_iota(I32, (tm, 1), 0)
    zm1 = jnp.where(row == 0, z_before, pltpu.roll(z, 1, 0))
    zp1 = jnp.where(row == tm - 1, z_after, pltpu.roll(z, tm - 1, 0))
    conv = bg_ref[...] * (zm1 * wc_ref[0:1, :] + z * wc_ref[1:2, :] + zp1 * wc_ref[2:3, :])
    mix = (jnp.dot(attn_ref[...], wo_ref[0:ATTN_WIDTH, :], preferred_element_type=F32)
           + jnp.dot(conv.astype(BF16), wo_ref[ATTN_WIDTH:, :], preferred_element_type=F32))
    x1 = x_ref[...] + mod_ref[0, 2:3, :] * _rms(mix, g_ref[0:1, :])
    x1_ref[...] = x1
    h2 = _rms(x1, g_ref[1:2, :]) * (1.0 + mod_ref[0, 4:5, :]) + mod_ref[0, 3:4, :]
    h2_ref[...] = h2.astype(BF16)
    _to_row_tiles(h2t_ref, h2)


def _outproj(attn, bg, z, w_conv, w_out_b, x2, mod, g2, seq, seq0, tm):
    t, d = x2.shape
    tps = seq // tm
    a = ATTN_WIDTH
    hb = tm // SUBLANES
    nh = t // SUBLANES
    row = lambda i: (i, 0)
    return pl.pallas_call(
        functools.partial(_outproj_kernel, tps=tps),
        grid=(t // tm,),
        in_specs=[pl.BlockSpec((tm, a), row),
                  pl.BlockSpec((tm, a), row),
                  pl.BlockSpec((tm, a), row),
                  pl.BlockSpec((SUBLANES, a), lambda i: (jnp.maximum(i * hb - 1, 0), 0)),
                  pl.BlockSpec((SUBLANES, a), lambda i: (jnp.minimum((i + 1) * hb, nh - 1), 0)),
                  pl.BlockSpec(w_conv.shape, lambda i: (0, 0)),
                  pl.BlockSpec(w_out_b.shape, lambda i: (0, 0)),
                  pl.BlockSpec((tm, d), row),
                  pl.BlockSpec((1, 6, d), lambda i: (seq0 + i // tps, 0, 0)),
                  pl.BlockSpec(g2.shape, lambda i: (0, 0))],
        out_specs=[pl.BlockSpec((tm, d), row),
                   pl.BlockSpec((tm, d), row),
                   pl.BlockSpec((tm * ROW_TILES, LANES), row)],
        out_shape=[jax.ShapeDtypeStruct((t, d), F32),
                   jax.ShapeDtypeStruct((t, d), BF16),
                   jax.ShapeDtypeStruct((t * ROW_TILES, LANES), F32)],
        compiler_params=_params(("arbitrary",)),
        name="outproj",
    )(attn, bg, z, z, z, w_conv, w_out_b, x2, mod, g2)


def _router_kernel(h_ref, wr_ref, b_ref, eid_ref, wgt_ref, rank_ref, cnt_ref, base_sc):
    i = pl.program_id(0)
    tt = h_ref.shape[0]

    @pl.when(i == 0)
    def _():
        base_sc[...] = jnp.zeros(base_sc.shape, F32)

    logits = lax.dot_general(wr_ref[...], h_ref[...], (((1,), (1,)), ((), ())), preferred_element_type=F32)
    scores = jax.nn.sigmoid(logits)
    sel = scores + b_ref[...]
    selg = sel.reshape(N_GROUPS, GROUP_SIZE, tt)
    ridx = lax.broadcasted_iota(I32, selg.shape, 1)
    m1 = jnp.max(selg, axis=1, keepdims=True)
    first = jnp.min(jnp.where(selg == m1, ridx, GROUP_SIZE), axis=1, keepdims=True)
    m2 = jnp.max(jnp.where(ridx == first, -jnp.inf, selg), axis=1, keepdims=True)
    gs = (m1 + m2).reshape(N_GROUPS, tt)
    gi = lax.broadcasted_iota(I32, gs.shape, 0)
    beaten = jnp.zeros(gs.shape, I32)
    for g in range(N_GROUPS):
        r = gs[g:g + 1, :]
        beaten = beaten + ((r > gs) | ((r == gs) & (g < gi))).astype(I32)
    gmask = (beaten < TOPK_GROUPS).reshape(N_GROUPS, 1, tt)
    masked = jnp.where(gmask, selg, -jnp.inf).reshape(N_EXPERTS, tt)

    eidx = lax.broadcasted_iota(I32, masked.shape, 0)
    picks, wts = [], []
    chosen = jnp.zeros(masked.shape, F32)
    for _ in range(TOP_K):
        mk = jnp.max(masked, axis=0, keepdims=True)
        pick = jnp.min(jnp.where(masked == mk, eidx, N_EXPERTS), axis=0, keepdims=True)
        hit = eidx == pick
        wts.append(jnp.sum(jnp.where(hit, scores, 0.0), axis=0, keepdims=True))
        masked = jnp.where(hit, -jnp.inf, masked)
        chosen = jnp.where(hit, 1.0, chosen)
        picks.append(pick)
    w = jnp.concatenate(wts, axis=0)
    w = w / (jnp.sum(w, axis=0, keepdims=True) + 1e-20) * ROUTED_SCALE

    t_src = lax.broadcasted_iota(I32, (tt, tt), 0)
    t_dst = lax.broadcasted_iota(I32, (tt, tt), 1)
    before = (t_src < t_dst).astype(BF16)
    prior = jnp.dot(chosen.astype(BF16), before, preferred_element_type=F32) + base_sc[...]
    ranks = [jnp.sum(jnp.where(eidx == pk, prior, 0.0), axis=0, keepdims=True) for pk in picks]

    eid_ref[...] = jnp.concatenate(picks, axis=0)
    wgt_ref[...] = w
    rank_ref[...] = jnp.concatenate(ranks, axis=0).astype(I32)
    base_sc[...] = base_sc[...] + jnp.sum(chosen, axis=1, keepdims=True)
    cnt_ref[...] = base_sc[...]


def _router(h2b, wr_t, bias_col, tt):
    t, d = h2b.shape
    col = lambda i: (0, i)
    return pl.pallas_call(
        _router_kernel,
        grid=(t // tt,),
        in_specs=[pl.BlockSpec((tt, d), lambda i: (i, 0)),
                  pl.BlockSpec(wr_t.shape, lambda i: (0, 0)),
                  pl.BlockSpec(bias_col.shape, lambda i: (0, 0))],
        out_specs=[pl.BlockSpec((TOP_K, tt), col), pl.BlockSpec((TOP_K, tt), col), pl.BlockSpec((TOP_K, tt), col),
                   pl.BlockSpec((N_EXPERTS, 1), lambda i: (0, 0))],
        out_shape=[jax.ShapeDtypeStruct((TOP_K, t), I32), jax.ShapeDtypeStruct((TOP_K, t), F32),
                   jax.ShapeDtypeStruct((TOP_K, t), I32), jax.ShapeDtypeStruct((N_EXPERTS, 1), F32)],
        scratch_shapes=[pltpu.VMEM((N_EXPERTS, 1), F32)],
        compiler_params=_params(("arbitrary",)),
        name="router",
    )(h2b, wr_t, bias_col)


def _dest_kernel(eid_ref, rank_ref, start_ref, dest_ref):
    tt = eid_ref.shape[1]
    eidx = lax.broadcasted_iota(I32, (N_EXPERTS, tt), 0)
    rows = []
    for k in range(TOP_K):
        hit = eidx == eid_ref[k:k + 1, :]
        rows.append(jnp.sum(jnp.where(hit, start_ref[...], 0.0), axis=0, keepdims=True))
    dest_ref[...] = jnp.concatenate(rows, axis=0).astype(I32) + rank_ref[...]


def _dest(eid, rank, start_col, tt):
    t = eid.shape[1]
    col = lambda i: (0, i)
    return pl.pallas_call(
        _dest_kernel,
        grid=(t // tt,),
        in_specs=[pl.BlockSpec((TOP_K, tt), col), pl.BlockSpec((TOP_K, tt), col),
                  pl.BlockSpec(start_col.shape, lambda i: (0, 0))],
        out_specs=pl.BlockSpec((TOP_K, tt), col),
        out_shape=jax.ShapeDtypeStruct((TOP_K, t), I32),
        compiler_params=_params(("arbitrary",)),
        name="dest",
    )(eid, rank, start_col)


def _dispatch_kernel(ps_ref, pn_ref, dest_hbm, h_ref, xs_hbm, idx_sm, zbuf, isem, rsem, zsem, *, td, te, epp):
    i = pl.program_id(0)
    n_steps = pl.num_programs(0)
    slot = i % 2
    n_rows = TOP_K * td
    bits = te.bit_length() - 1

    def idx_copy(blk, sl):
        return pltpu.make_async_copy(dest_hbm.at[blk], idx_sm.at[sl], isem.at[sl])

    @pl.when(i == 0)
    def _():
        zbuf[...] = jnp.zeros(zbuf.shape, F32)
        idx_copy(0, 0).start()

    idx_copy(i, slot).wait()

    @pl.when(i + 1 < n_steps)
    def _():
        idx_copy(i + 1, 1 - slot).start()

    for j in range(td):
        for k in range(TOP_K):
            pltpu.make_async_copy(_row_tiles(h_ref, j), _row_tiles(xs_hbm, idx_sm[slot, 0, k * td + j]), rsem).start()

    def pad_copies(action):
        for ee in range(epp):
            e = i * epp + ee
            ec = jnp.minimum(e, N_EXPERTS - 1)
            n_pad = jnp.where(e < N_EXPERTS, pn_ref[ec], 0)
            for b in range(bits):
                size = 1 << b
                off = ps_ref[ec] + ((n_pad >> (b + 1)) << (b + 1))

                @pl.when(((n_pad >> b) & 1) == 1)
                def _():
                    action(pltpu.make_async_copy(_row_tiles(zbuf, 0, size), _row_tiles(xs_hbm, off, size), zsem))

    pad_copies(lambda c: c.start())
    pltpu.make_async_copy(_row_tiles(xs_hbm, 0, n_rows), _row_tiles(xs_hbm, 0, n_rows), rsem).wait()
    pad_copies(lambda c: c.wait())


def _dispatch(pad_start, pad_len, dest_rows, h2t, n_rows_total, td, te):
    t = h2t.shape[0] // ROW_TILES
    n_steps = t // td
    epp = -(-N_EXPERTS // n_steps)
    grid_spec = pltpu.PrefetchScalarGridSpec(
        num_scalar_prefetch=2,
        grid=(n_steps,),
        in_specs=[pl.BlockSpec(memory_space=pl.ANY),
                  pl.BlockSpec((td * ROW_TILES, LANES), lambda i, ps, pn: (i, 0))],
        out_specs=pl.BlockSpec(memory_space=pl.ANY),
        scratch_shapes=[pltpu.SMEM((2, 1, TOP_K * td), I32),
                        pltpu.VMEM((te // 2 * ROW_TILES, LANES), F32),
                        pltpu.SemaphoreType.DMA((2,)), pltpu.SemaphoreType.DMA, pltpu.SemaphoreType.DMA],
    )
    return pl.pallas_call(
        functools.partial(_dispatch_kernel, td=td, te=te, epp=epp),
        grid_spec=grid_spec,
        out_shape=jax.ShapeDtypeStruct((n_rows_total * ROW_TILES, LANES), F32),
        compiler_params=_params(("arbitrary",)),
        name="dispatch",
    )(pad_start, pad_len, dest_rows, h2t)


def _experts_kernel(be_ref, nu_ref, xs_ref, wg_ref, wu_ref, wd_ref, y_ref, wgb, wub, wdb):
    i = pl.program_id(0)

    @pl.when(i < nu_ref[0])
    def _():
        fresh = (i == 0) | (be_ref[i] != be_ref[jnp.maximum(i - 1, 0)])

        @pl.when(fresh)
        def _():
            wgb[...] = wg_ref[0].astype(BF16)
            wub[...] = wu_ref[0].astype(BF16)
            wdb[...] = wd_ref[0].astype(BF16)

        x = _from_row_tiles(xs_ref, 0, xs_ref.shape[0] // ROW_TILES).astype(BF16)
        g = jnp.dot(x, wgb[...], preferred_element_type=F32)
        u = jnp.dot(x, wub[...], preferred_element_type=F32)
        a = (g * jax.nn.sigmoid(g) * u).astype(BF16)
        _to_row_tiles(y_ref, jnp.dot(a, wdb[...], preferred_element_type=F32))


def _experts(block_e, n_used, xs, w_gate, w_up, w_down, te):
    n_blocks = xs.shape[0] // (te * ROW_TILES)
    d, hdn = w_gate.shape[1], w_gate.shape[2]
    wmap = lambda i, be, nu: (be[i], 0, 0)
    rmap = lambda i, be, nu: (jnp.minimum(i, nu[0] - 1), 0)
    grid_spec = pltpu.PrefetchScalarGridSpec(
        num_scalar_prefetch=2,
        grid=(n_blocks,),
        in_specs=[pl.BlockSpec((te * ROW_TILES, LANES), rmap),
                  pl.BlockSpec((1, d, hdn), wmap),
                  pl.BlockSpec((1, d, hdn), wmap),
                  pl.BlockSpec((1, hdn, d), wmap)],
        out_specs=pl.BlockSpec((te * ROW_TILES, LANES), rmap),
        scratch_shapes=[pltpu.VMEM((d, hdn), BF16), pltpu.VMEM((d, hdn), BF16), pltpu.VMEM((hdn, d), BF16)],
    )
    return pl.pallas_call(
        _experts_kernel,
        grid_spec=grid_spec,
        out_shape=jax.ShapeDtypeStruct(xs.shape, F32),
        compiler_params=_params(("arbitrary",)),
        name="experts",
    )(block_e, n_used, xs, w_gate, w_up, w_down)


def _combine_kernel(dest_hbm, y_hbm, w_ref, h_ref, sg_ref, su_ref, sd_ref, x1_ref, mod_ref, g_ref, o_ref,
                    idx_sm, ybuf, isem, ysem, *, tc):
    i = pl.program_id(0)
    n_steps = pl.num_programs(0)
    slot = i % 2
    nslot = 1 - slot
    n_rows = TOP_K * tc

    def idx_copy(blk, sl):
        return pltpu.make_async_copy(dest_hbm.at[blk], idx_sm.at[sl], isem.at[sl])

    def issue_rows(sl):
        for r in range(n_rows):
            pltpu.make_async_copy(_row_tiles(y_hbm, idx_sm[sl, 0, r]), _row_tiles(ybuf.at[sl], r), ysem.at[sl]).start()

    @pl.when(i == 0)
    def _():
        idx_copy(0, 0).start()
        idx_copy(0, 0).wait()
        issue_rows(0)

        @pl.when(n_steps > 1)
        def _():
            idx_copy(1, 1).start()

    @pl.when(i + 1 < n_steps)
    def _():
        idx_copy(i + 1, nslot).wait()
        issue_rows(nslot)

        @pl.when(i + 2 < n_steps)
        def _():
            idx_copy(i + 2, slot).start()

    hb = h_ref[...]
    sg = jnp.dot(hb, sg_ref[...], preferred_element_type=F32)
    su = jnp.dot(hb, su_ref[...], preferred_element_type=F32)
    f = jnp.dot((sg * jax.nn.sigmoid(sg) * su).astype(BF16), sd_ref[...], preferred_element_type=F32)

    pltpu.make_async_copy(_row_tiles(y_hbm, 0, n_rows), ybuf.at[slot], ysem.at[slot]).wait()
    yv = ybuf.at[slot]
    w = w_ref[...]
    for k in range(TOP_K):
        f = f + _from_row_tiles(yv, k * tc, tc) * w[:, k:k + 1]
    o_ref[...] = x1_ref[...] + mod_ref[0, 5:6, :] * _rms(f, g_ref[...])


def _combine(dest_rows, y3, w_tok, h2b, sg_b, su_b, sd_b, x1, mod, g_post, seq_of_tile, tc):
    t, d = x1.shape
    row = lambda i: (i, 0)
    full = lambda i: (0, 0)
    return pl.pallas_call(
        functools.partial(_combine_kernel, tc=tc),
        grid=(t // tc,),
        in_specs=[pl.BlockSpec(memory_space=pl.ANY),
                  pl.BlockSpec(memory_space=pl.ANY),
                  pl.BlockSpec((tc, TOP_K), row),
                  pl.BlockSpec((tc, d), row),
                  pl.BlockSpec(sg_b.shape, full),
                  pl.BlockSpec(su_b.shape, full),
                  pl.BlockSpec(sd_b.shape, full),
                  pl.BlockSpec((tc, d), row),
                  pl.BlockSpec((1, 6, d), lambda i: (seq_of_tile(i), 0, 0)),
                  pl.BlockSpec(g_post.shape, full)],
        out_specs=pl.BlockSpec((tc, d), row),
        out_shape=jax.ShapeDtypeStruct((t, d), F32),
        scratch_shapes=[pltpu.SMEM((2, 1, TOP_K * tc), I32),
                        pltpu.VMEM((2, TOP_K * tc * ROW_TILES, LANES), F32),
                        pltpu.SemaphoreType.DMA((2,)), pltpu.SemaphoreType.DMA((2,))],
        compiler_params=_params(("arbitrary",)),
        name="combine",
    )(dest_rows, y3, w_tok, h2b, sg_b, su_b, sd_b, x1, mod, g_post)


def _pick(n, pref):
    t = min(n, pref)
    assert n % t == 0, (n, pref)
    return t


def _layer(xs, cs, w_ada, b_ada, g_pre_mix, g_post_mix, g_pre_ffn, g_post_ffn, w_in, w_conv,
           lam_q1, lam_k1, lam_q2, lam_k2, g_head, w_out, w_router, router_bias,
           w_exp_gate, w_exp_up, w_exp_down, w_sh_gate, w_sh_up, w_sh_down):
    d = D_MODEL
    n_seq = sum(c.shape[0] for c in cs)
    pad = -n_seq % SUBLANES
    c_all = jnp.concatenate(list(cs) + [jnp.zeros((pad, d), F32)], axis=0)
    mod = _ada(c_all, w_ada, b_ada.reshape(1, -1)).reshape(n_seq + pad, 6, d)

    w_in_b = w_in.astype(BF16)
    w_out_b = w_out.astype(BF16)
    lam_vecs = jnp.stack([lam_q1, lam_k1, lam_q2, lam_k2]).astype(F32)
    g_pre = g_pre_mix.reshape(1, d)
    g2 = jnp.stack([g_post_mix, g_pre_ffn])
    g_hd = g_head.reshape(1, HEAD_WIDTH)

    x1s, h2bs, h2ts = [], [], []
    seq0 = 0
    for x in xs:
        b, s, _ = x.shape
        x2 = x.reshape(b * s, d)
        tm = _pick(s, 512)
        q, k, v, bg, z = _inproj(x2, mod, g_pre, w_in_b, _rope_tables(s), s, seq0, tm)
        shp = (b, s, ATTN_WIDTH)
        attn = _attn(lam_vecs, g_hd, q.reshape(shp), k.reshape(shp), v.reshape(shp), _pick(s, 512), _pick(s // 4, 1024))
        x1, h2b, h2t = _outproj(attn.reshape(b * s, ATTN_WIDTH), bg, z, w_conv, w_out_b, x2, mod, g2, s, seq0, tm)
        x1s.append(x1)
        h2bs.append(h2b)
        h2ts.append(h2t)
        seq0 += b

    x1 = jnp.concatenate(x1s, axis=0)
    h2b = jnp.concatenate(h2bs, axis=0)
    h2t = jnp.concatenate(h2ts, axis=0)
    t = x1.shape[0]

    tt = _pick(t, 512)
    eid, wgt, rank, counts = _router(h2b, w_router.T.astype(BF16), router_bias.reshape(N_EXPERTS, 1), tt)

    te = EXPERT_BLOCK_ROWS
    tk_total = t * TOP_K
    n_blocks = tk_total // te + N_EXPERTS
    counts = counts.reshape(N_EXPERTS).astype(I32)
    padded = (counts + te - 1) // te * te
    end_pad = jnp.cumsum(padded)
    start_pad = end_pad - padded
    n_used = (end_pad[-1:] // te).astype(I32)
    block_first = jnp.arange(n_blocks, dtype=I32) * te
    block_e = jnp.sum((end_pad[None, :] <= block_first[:, None]).astype(I32), axis=1)
    block_e = jnp.minimum(block_e, block_e[n_used[0] - 1])
    dest = _dest(eid, rank, start_pad.astype(F32).reshape(N_EXPERTS, 1), tt)

    def tile_rows(tile):
        n = t // tile
        return dest.reshape(TOP_K, n, tile).transpose(1, 0, 2).reshape(n, 1, TOP_K * tile)

    td = _pick(t, 256)
    x_sorted = _dispatch(start_pad + counts, padded - counts, tile_rows(td), h2t, n_blocks * te, td, te)
    y3 = _experts(block_e, n_used, x_sorted, w_exp_gate, w_exp_up, w_exp_down, te)

    tc = _pick(t, 128)
    dest_rows = tile_rows(tc)
    per0, per1 = xs[0].shape[1] // tc, xs[1].shape[1] // tc
    split = xs[0].shape[0] * per0

    def seq_of_tile(i):
        return jnp.where(i < split, i // per0, xs[0].shape[0] + (i - split) // per1)

    out = _combine(dest_rows, y3, wgt.T, h2b, w_sh_gate.astype(BF16), w_sh_up.astype(BF16), w_sh_down.astype(BF16),
                   x1, mod, g_post_ffn.reshape(1, d), seq_of_tile, tc)

    outs, off = [], 0
    for x in xs:
        n = x.shape[0] * x.shape[1]
        outs.append(out[off:off + n].reshape(x.shape))
        off += n
    return tuple(outs)


def kernel(x_prompt, x_sample, c_prompt, c_sample, w_ada, b_ada, g_pre_mix, g_post_mix, g_pre_ffn, g_post_ffn, w_in, w_conv, lam_q1, lam_k1, lam_q2, lam_k2, g_head, w_out, w_router, router_bias, w_exp_gate, w_exp_up, w_exp_down, w_sh_gate, w_sh_up, w_sh_down):
    return _layer((x_prompt, x_sample), (c_prompt, c_sample), w_ada[0], b_ada[0], g_pre_mix[0], g_post_mix[0],
                  g_pre_ffn[0], g_post_ffn[0], w_in[0], w_conv[0], lam_q1[0], lam_k1[0], lam_q2[0], lam_k2[0],
                  g_head[0], w_out[0], w_router[0], router_bias[0], w_exp_gate[0], w_exp_up[0], w_exp_down[0],
                  w_sh_gate[0], w_sh_up[0], w_sh_down[0])
```
